```python
import math
import jax, jax.numpy as jnp
from jax import lax
import numpy as np

D_MODEL = 2048
BATCH = 4
SEQ = 4096
DEPTH = 1

DIFF_HEADS = 8
DIFF_HEAD_DIM = 64
DIFF_V_DIM = 2 * DIFF_HEAD_DIM
DIFF_WIDTH = DIFF_HEADS * DIFF_V_DIM
QK_WIDTH = DIFF_HEADS * 2 * DIFF_HEAD_DIM
CONV_WIDTH = D_MODEL - DIFF_WIDTH
CONV_GROUPS = 8
CONV_KERNEL = 31
MIX_WIDTH = DIFF_WIDTH + CONV_WIDTH
IN_WIDTH = 2 * QK_WIDTH + DIFF_WIDTH + 2 * CONV_WIDTH
D_FF = -(-(8 * D_MODEL) // (3 * 256)) * 256
REL_BUCKETS = 32
REL_MAX_DIST = 128
Q_BLOCK = 128
EPS = 1e-6
MASK_VALUE = -1e30

kernel_name = "hymba_diffattn_conformer_hybrid"


def rmsnorm(x, g):
    xf = x.astype(jnp.float32)
    y = xf * lax.rsqrt(jnp.mean(xf * xf, axis=-1, keepdims=True) + EPS)
    return (y * g.astype(jnp.float32)).astype(x.dtype)


def layernorm(x, g, b):
    xf = x.astype(jnp.float32)
    mu = jnp.mean(xf, axis=-1, keepdims=True)
    var = jnp.mean(jnp.square(xf - mu), axis=-1, keepdims=True)
    y = (xf - mu) * lax.rsqrt(var + EPS)
    return (y * g.astype(jnp.float32) + b.astype(jnp.float32)).astype(x.dtype)


def t5_bucket(q_pos, k_pos):
    n = jnp.maximum(q_pos[:, None] - k_pos[None, :], 0)
    max_exact = REL_BUCKETS // 2
    nf = jnp.maximum(n, 1).astype(jnp.float32)
    large = max_exact + (jnp.log(nf / max_exact) / math.log(REL_MAX_DIST / max_exact)
                         * (REL_BUCKETS - max_exact)).astype(jnp.int32)
    large = jnp.minimum(large, REL_BUCKETS - 1)
    return jnp.where(n < max_exact, n, large)


def diff_attention(q, k, v, rel_table, lam, sub_g, lam_init):
    B, S = q.shape[0], q.shape[1]
    nblk = S // Q_BLOCK
    qb = q.reshape(B, nblk, Q_BLOCK, DIFF_HEADS, 2, DIFF_HEAD_DIM).transpose(1, 0, 2, 3, 4, 5)
    k_pos = jnp.arange(S)
    scale = DIFF_HEAD_DIM ** -0.5

    def one_block(args):
        q_blk, i = args
        q_pos = i * Q_BLOCK + jnp.arange(Q_BLOCK)
        s = jnp.einsum('bqhcd,bkhcd->bhcqk', q_blk, k).astype(jnp.float32) * scale
        bias = rel_table[t5_bucket(q_pos, k_pos)].astype(jnp.float32)
        s = s + jnp.transpose(bias, (2, 0, 1))[None, :, None]
        mask = k_pos[None, :] <= q_pos[:, None]
        s = jnp.where(mask, s, MASK_VALUE)
        p = jax.nn.softmax(s, axis=-1)
        a = p[:, :, 0] - lam * p[:, :, 1]
        return jnp.einsum('bhqk,bkhe->bqhe', a.astype(v.dtype), v)

    o = lax.map(one_block, (qb, jnp.arange(nblk)))
    o = o.transpose(1, 0, 2, 3, 4).reshape(B, S, DIFF_HEADS, DIFF_V_DIM)
    o = rmsnorm(o, sub_g) * (1.0 - lam_init)
    return o.reshape(B, S, DIFF_WIDTH)


def conformer_conv(u, dw_w, dw_b, ln_g, ln_b):
    a, gate = jnp.split(u, 2, axis=-1)
    h = a * jax.nn.sigmoid(gate)
    h = jnp.pad(h, ((0, 0), (CONV_KERNEL - 1, 0), (0, 0)))
    h = lax.conv_general_dilated(h, dw_w[:, None, :].astype(h.dtype), window_strides=(1,),
                                 padding='VALID', dimension_numbers=('NWC', 'WIO', 'NWC'),
                                 feature_group_count=CONV_WIDTH) + dw_b
    h = layernorm(h, ln_g, ln_b)
    return jax.nn.silu(h)


def setup_inputs(seed: int = 0) -> dict:
    key = jax.random.key(seed)
    ks = jax.random.split(key, 24)
    f32 = jnp.float32
    L = DEPTH

    def nrm(k, shape, scale):
        return jax.random.normal(k, shape, f32) * scale

    def gain(k, shape):
        return 1.0 + 0.01 * jax.random.normal(k, shape, f32)

    return {
        "x": jax.random.normal(ks[0], (BATCH, SEQ, D_MODEL), f32),
        "w_in": nrm(ks[1], (L, D_MODEL, IN_WIDTH), D_MODEL ** -0.5),
        "lambda_q1": nrm(ks[2], (L, DIFF_HEAD_DIM), 0.1),
        "lambda_k1": nrm(ks[3], (L, DIFF_HEAD_DIM), 0.1),
        "lambda_q2": nrm(ks[4], (L, DIFF_HEAD_DIM), 0.1),
        "lambda_k2": nrm(ks[5], (L, DIFF_HEAD_DIM), 0.1),
        "subln_g": gain(ks[6], (L, DIFF_V_DIM)),
        "conv_dw_w": nrm(ks[7], (L, CONV_KERNEL, CONV_WIDTH), CONV_KERNEL ** -0.5),
        "conv_dw_b": nrm(ks[8], (L, CONV_WIDTH), 0.02),
        "conv_ln_g": gain(ks[9], (L, CONV_WIDTH)),
        "conv_ln_b": nrm(ks[10], (L, CONV_WIDTH), 0.02),
        "w_out": nrm(ks[11], (L, MIX_WIDTH, D_MODEL), MIX_WIDTH ** -0.5),
        "rel_bias": nrm(ks[12], (REL_BUCKETS, DIFF_HEADS), 0.5),
        "norm_pre_mix": gain(ks[13], (L, D_MODEL)),
        "norm_post_mix": gain(ks[14], (L, D_MODEL)),
        "norm_pre_ffn": gain(ks[15], (L, D_MODEL)),
        "norm_post_ffn": gain(ks[16], (L, D_MODEL)),
        "w_gate": nrm(ks[17], (L, D_MODEL, D_FF), D_MODEL ** -0.5),
        "w_up": nrm(ks[18], (L, D_MODEL, D_FF), D_MODEL ** -0.5),
        "w_down": nrm(ks[19], (L, D_FF, D_MODEL), D_FF ** -0.5),
    }


def reference(x, w_in, lambda_q1, lambda_k1, lambda_q2, lambda_k2, subln_g,
              conv_dw_w, conv_dw_b, conv_ln_g, conv_ln_b, w_out, rel_bias,
              norm_pre_mix, norm_post_mix, norm_pre_ffn, norm_post_ffn,
              w_gate, w_up, w_down):
    B, S = x.shape[0], x.shape[1]
    for l in range(DEPTH):
        lam_init = 0.8 - 0.6 * math.exp(-0.3 * l)
        h = rmsnorm(x, norm_pre_mix[l])
        y = jnp.einsum('bsd,de->bse', h, w_in[l])
        q = y[..., :QK_WIDTH].reshape(B, S, DIFF_HEADS, 2, DIFF_HEAD_DIM)
        k = y[..., QK_WIDTH:2 * QK_WIDTH].reshape(B, S, DIFF_HEADS, 2, DIFF_HEAD_DIM)
        v = y[..., 2 * QK_WIDTH:2 * QK_WIDTH + DIFF_WIDTH].reshape(B, S, DIFF_HEADS, DIFF_V_DIM)
        u = y[..., 2 * QK_WIDTH + DIFF_WIDTH:]
        lam = (jnp.exp(jnp.sum(lambda_q1[l] * lambda_k1[l]).astype(jnp.float32))
               - jnp.exp(jnp.sum(lambda_q2[l] * lambda_k2[l]).astype(jnp.float32))
               + lam_init)
        o_attn = diff_attention(q, k, v, rel_bias, lam, subln_g[l], lam_init)
        o_conv = conformer_conv(u, conv_dw_w[l], conv_dw_b[l], conv_ln_g[l], conv_ln_b[l])
        mix = jnp.concatenate([o_attn, o_conv.astype(o_attn.dtype)], axis=-1)
        mix = jnp.einsum('bse,ed->bsd', mix, w_out[l])
        x = x + rmsnorm(mix, norm_post_mix[l])
        h = rmsnorm(x, norm_pre_ffn[l])
        g = jnp.einsum('bsd,df->bsf', h, w_gate[l])
        up = jnp.einsum('bsd,df->bsf', h, w_up[l])
        f = jnp.einsum('bsf,fd->bsd', jax.nn.silu(g) * up, w_down[l])
        x = x + rmsnorm(f, norm_post_ffn[l])
    return x
```

```python
import functools
import math

import jax
import jax.numpy as jnp
from jax import lax
from jax.experimental import pallas as pl
from jax.experimental.pallas import tpu as pltpu

D_MODEL = 2048
DIFF_HEADS = 8
DIFF_HEAD_DIM = 64
DIFF_V_DIM = 2 * DIFF_HEAD_DIM
DIFF_WIDTH = DIFF_HEADS * DIFF_V_DIM
QK_WIDTH = DIFF_HEADS * 2 * DIFF_HEAD_DIM
CONV_WIDTH = D_MODEL - DIFF_WIDTH
CONV_KERNEL = 31
D_FF = 5632
REL_BUCKETS = 32
REL_MAX_DIST = 128
EPS = 1e-6
MASK_VALUE = -1e30
LAM_INIT = 0.8 - 0.6 * math.exp(-0.3 * 0)

LANES = 128
SUBLANES = 8
VMEM_LIMIT = 56 * 1024 * 1024

ATT_TILE = 256
PROJ_ROWS = 1024
PROJ_COLS = 1024
CONV_ROWS = 512
CONV_CHUNK = 32
CONV_HALO = 32
OUT_ROWS = 512
FFN_ROWS = 512
FFN_COLS = 512

N_CTILES = CONV_WIDTH // LANES

f32 = jnp.float32
bf16 = jnp.bfloat16


def _params(*sem):
    return pltpu.CompilerParams(dimension_semantics=sem, vmem_limit_bytes=VMEM_LIMIT)


def _rms(v, g):
    return v * lax.rsqrt(jnp.mean(v * v, axis=-1, keepdims=True) + EPS) * g


def _bias_kernel(rel_ref, e_ref):
    h = pl.program_id(0)
    t = e_ref.shape[1]
    a = lax.broadcasted_iota(jnp.int32, (t, 2 * t), 0)
    b = lax.broadcasted_iota(jnp.int32, (t, 2 * t), 1)
    n = a - b + t
    nn = jnp.maximum(n, 0)
    max_exact = REL_BUCKETS // 2
    nf = jnp.maximum(nn, 1).astype(f32)
    large = max_exact + (jnp.log(nf * (1.0 / max_exact)) / math.log(REL_MAX_DIST / max_exact)
                         * (REL_BUCKETS - max_exact)).astype(jnp.int32)
    large = jnp.minimum(large, REL_BUCKETS - 1)
    bucket = jnp.where(nn < max_exact, nn, large)
    val = jnp.full((t, 2 * t), rel_ref[0, h], f32)
    for bk in range(1, REL_BUCKETS):
        val = jnp.where(bucket == bk, rel_ref[bk, h], val)
    val = val - rel_ref[REL_BUCKETS - 1, h]
    e_ref[0] = jnp.where(n < 0, MASK_VALUE, val)


def _bias_tiles(rel_bias):
    t = ATT_TILE
    assert t >= REL_MAX_DIST
    return pl.pallas_call(
        _bias_kernel,
        out_shape=jax.ShapeDtypeStruct((DIFF_HEADS, t, 2 * t), f32),
        grid=(DIFF_HEADS,),
        in_specs=[pl.BlockSpec(memory_space=pltpu.SMEM)],
        out_specs=pl.BlockSpec((1, t, 2 * t), lambda h: (h, 0, 0)),
        compiler_params=_params("arbitrary"),
        name="bias_tiles",
    )(rel_bias)


def _in_proj_kernel(x_ref, g_ref, w_ref, qkv_ref, glu_ref, hn_ref):
    j = pl.program_id(1)

    @pl.when(j == 0)
    def _():
        hn_ref[...] = _rms(x_ref[...], g_ref[...]).astype(bf16)

    def proj():
        return jnp.dot(hn_ref[...], w_ref[...], preferred_element_type=f32)

    @pl.when(j == 0)
    def _():
        qkv_ref[...] = (proj() * (DIFF_HEAD_DIM ** -0.5)).astype(bf16)

    @pl.when(jnp.logical_or(j == 1, j == 2))
    def _():
        qkv_ref[...] = proj().astype(bf16)

    @pl.when(j >= 3)
    def _():
        y = proj()
        half = PROJ_COLS // 2
        glu = y[:, :half] * jax.nn.sigmoid(y[:, half:])
        for c in range(half // LANES):
            glu_ref[c] = glu[:, c * LANES:(c + 1) * LANES]


def _in_proj(x2, g, w_in_bf):
    m = x2.shape[0]
    n_chunks = w_in_bf.shape[1] // PROJ_COLS
    half_tiles = PROJ_COLS // 2 // LANES
    return pl.pallas_call(
        _in_proj_kernel,
        out_shape=(jax.ShapeDtypeStruct((m, 3 * QK_WIDTH), bf16),
                   jax.ShapeDtypeStruct((N_CTILES, m, LANES), f32)),
        grid=(m // PROJ_ROWS, n_chunks),
        in_specs=[pl.BlockSpec((PROJ_ROWS, D_MODEL), lambda i, j: (i, 0)),
                  pl.BlockSpec((1, D_MODEL), lambda i, j: (0, 0)),
                  pl.BlockSpec((D_MODEL, PROJ_COLS), lambda i, j: (0, j))],
        out_specs=(pl.BlockSpec((PROJ_ROWS, PROJ_COLS), lambda i, j: (i, jnp.minimum(j, 2))),
                   pl.BlockSpec((half_tiles, PROJ_ROWS, LANES),
                                lambda i, j: (jnp.maximum(j - 3, 0), i, 0))),
        scratch_shapes=[pltpu.VMEM((PROJ_ROWS, D_MODEL), bf16)],
        compiler_params=_params("arbitrary", "arbitrary"),
        name="in_proj",
    )(x2, g, w_in_bf)


def _softmax_tile(qq, k, v, bias, m, l, acc):
    s = lax.dot_general(qq, k, (((1,), (1,)), ((), ())), preferred_element_type=f32)
    if bias is not None:
        t = bias.shape[0]
        s = (s.reshape(2, t, s.shape[1]) + bias[None]).reshape(2 * t, s.shape[1])
    m_new = jnp.maximum(m, jnp.max(s, axis=-1, keepdims=True))
    alpha = jnp.exp(m - m_new)
    p = jnp.exp(s - m_new)
    l = alpha * l + jnp.sum(p, axis=-1, keepdims=True)
    acc = alpha * acc + jnp.dot(p.astype(bf16), v, preferred_element_type=f32)
    return m_new, l, acc


def _attn_kernel(q_ref, k_ref, v_ref, e_ref, lamv_ref, subg_ref, o_ref):
    t = ATT_TILE
    i = pl.program_id(2)
    q = q_ref[...]
    lane = lax.broadcasted_iota(jnp.int32, q.shape, 1)
    zero = jnp.zeros_like(q)
    qq = jnp.concatenate([jnp.where(lane < DIFF_HEAD_DIM, q, zero),
                          jnp.where(lane >= DIFF_HEAD_DIM, q, zero)], axis=0)

    def kv(j):
        start = pl.multiple_of(j * t, t)
        return k_ref[pl.ds(start, t), :], v_ref[pl.ds(start, t), :]

    k, v = kv(i)
    s = lax.dot_general(qq, k, (((1,), (1,)), ((), ())), preferred_element_type=f32)
    s = (s.reshape(2, t, t) + e_ref[0, :, t:][None]).reshape(2 * t, t)
    m = jnp.max(s, axis=-1, keepdims=True)
    p = jnp.exp(s - m)
    l = jnp.sum(p, axis=-1, keepdims=True)
    acc = jnp.dot(p.astype(bf16), v, preferred_element_type=f32)

    k, v = kv(jnp.maximum(i - 1, 0))
    bias = jnp.where(i == 0, MASK_VALUE, e_ref[0, :, :t])
    m, l, acc = _softmax_tile(qq, k, v, bias, m, l, acc)

    def far(j, carry):
        k, v = kv(j)
        return _softmax_tile(qq, k, v, None, *carry)

    m, l, acc = lax.fori_loop(0, jnp.maximum(i - 1, 0), far, (m, l, acc))

    lv = lamv_ref[...]
    lam = (jnp.exp(jnp.sum(lv[0:1] * lv[1:2], axis=-1, keepdims=True))
           - jnp.exp(jnp.sum(lv[2:3] * lv[3:4], axis=-1, keepdims=True)) + LAM_INIT)
    o = acc / l
    o = o[:t] - lam * o[t:]
    o_ref[...] = (_rms(o, subg_ref[...]) * (1.0 - LAM_INIT)).astype(bf16)


def _attention(qkv, e_tiles, lamv, subg, batch, seq):
    t = ATT_TILE
    nq = seq // t
    h_blocks = QK_WIDTH // LANES
    return pl.pallas_call(
        _attn_kernel,
        out_shape=jax.ShapeDtypeStruct((batch * seq, DIFF_WIDTH), bf16),
        grid=(batch, DIFF_HEADS, nq),
        in_specs=[pl.BlockSpec((t, LANES), lambda b, h, i: (b * nq + i, h)),
                  pl.BlockSpec((seq, LANES), lambda b, h, i: (b, h_blocks + h)),
                  pl.BlockSpec((seq, LANES), lambda b, h, i: (b, 2 * h_blocks + h)),
                  pl.BlockSpec((1, t, 2 * t), lambda b, h, i: (h, 0, 0)),
                  pl.BlockSpec((4, DIFF_HEAD_DIM), lambda b, h, i: (0, 0)),
                  pl.BlockSpec((1, DIFF_V_DIM), lambda b, h, i: (0, 0))],
        out_specs=pl.BlockSpec((t, LANES), lambda b, h, i: (b * nq + i, h)),
        compiler_params=_params("arbitrary", "arbitrary", "arbitrary"),
        name="attention",
    )(qkv, qkv, qkv, e_tiles, lamv, subg)


def _conv_kernel(main_ref, halo_ref, w_ref, b_ref, g_ref, beta_ref, o_ref, buf_ref, *, tiles_per_seq):
    i = pl.program_id(0)
    first = (i % tiles_per_seq) == 0
    halo = halo_ref[...]
    buf_ref[:, :CONV_HALO, :] = jnp.where(first, jnp.zeros_like(halo), halo)
    buf_ref[:, CONV_HALO:, :] = main_ref[...]
    lead = CONV_HALO - (CONV_KERNEL - 1)

    def chunk(r, _):
        base = pl.multiple_of(r * CONV_CHUNK, CONV_CHUNK)
        cols = []
        for c in range(N_CTILES):
            acc = jnp.broadcast_to(b_ref[:, c * LANES:(c + 1) * LANES], (CONV_CHUNK, LANES))
            for tap in range(CONV_KERNEL):
                wt = w_ref[tap:tap + 1, c * LANES:(c + 1) * LANES]
                acc = acc + wt * buf_ref[c, pl.ds(base + lead + tap, CONV_CHUNK), :]
            cols.append(acc)
        y = jnp.concatenate(cols, axis=-1)
        mu = jnp.mean(y, axis=-1, keepdims=True)
        yc = y - mu
        var = jnp.mean(yc * yc, axis=-1, keepdims=True)
        z = yc * lax.rsqrt(var + EPS) * g_ref[...] + beta_ref[...]
        o_ref[pl.ds(base, CONV_CHUNK), :] = (z * jax.nn.sigmoid(z)).astype(bf16)
        return 0

    lax.fori_loop(0, CONV_ROWS // CONV_CHUNK, chunk, 0)


def _conv(glu, dw_w, dw_b, ln_g, ln_b, seq):
    m = glu.shape[1]
    halo_blocks = CONV_ROWS // CONV_HALO
    w_pad = jnp.pad(dw_w, ((0, 32 - CONV_KERNEL), (0, 0)))
    return pl.pallas_call(
        functools.partial(_conv_kernel, tiles_per_seq=seq // CONV_ROWS),
        out_shape=jax.ShapeDtypeStruct((m, CONV_WIDTH), bf16),
        grid=(m // CONV_ROWS,),
        in_specs=[pl.BlockSpec((N_CTILES, CONV_ROWS, LANES), lambda i: (0, i, 0)),
                  pl.BlockSpec((N_CTILES, CONV_HALO, LANES),
                               lambda i: (0, jnp.maximum(i * halo_blocks - 1, 0), 0)),
                  pl.BlockSpec((32, CONV_WIDTH), lambda i: (0, 0)),
                  pl.BlockSpec((1, CONV_WIDTH), lambda i: (0, 0)),
                  pl.BlockSpec((1, CONV_WIDTH), lambda i: (0, 0)),
                  pl.BlockSpec((1, CONV_WIDTH), lambda i: (0, 0))],
        out_specs=pl.BlockSpec((CONV_ROWS, CONV_WIDTH), lambda i: (i, 0)),
        scratch_shapes=[pltpu.VMEM((N_CTILES, CONV_HALO + CONV_ROWS, LANES), f32)],
        compiler_params=_params("arbitrary"),
        name="conv",
    )(glu, glu, w_pad, dw_b, ln_g, ln_b)


def _out_proj_kernel(oa_ref, oc_ref, wa_ref, wc_ref, x_ref, g_ref, o_ref):
    mix = (jnp.dot(oa_ref[...], wa_ref[...], preferred_element_type=f32)
           + jnp.dot(oc_ref[...], wc_ref[...], preferred_element_type=f32))
    o_ref[...] = x_ref[...] + _rms(mix, g_ref[...])


def _out_proj(o_attn, o_conv, w_out_bf, x2, g):
    m = x2.shape[0]
    return pl.pallas_call(
        _out_proj_kernel,
        out_shape=jax.ShapeDtypeStruct((m, D_MODEL), f32),
        grid=(m // OUT_ROWS,),
        in_specs=[pl.BlockSpec((OUT_ROWS, DIFF_WIDTH), lambda i: (i, 0)),
                  pl.BlockSpec((OUT_ROWS, CONV_WIDTH), lambda i: (i, 0)),
                  pl.BlockSpec((DIFF_WIDTH, D_MODEL), lambda i: (0, 0)),
                  pl.BlockSpec((CONV_WIDTH, D_MODEL), lambda i: (1, 0)),
                  pl.BlockSpec((OUT_ROWS, D_MODEL), lambda i: (i, 0)),
                  pl.BlockSpec((1, D_MODEL), lambda i: (0, 0))],
        out_specs=pl.BlockSpec((OUT_ROWS, D_MODEL), lambda i: (i, 0)),
        compiler_params=_params("arbitrary"),
        name="out_proj",
    )(o_attn, o_conv, w_out_bf, w_out_bf, x2, g)


def _ffn_kernel(x_ref, gpre_ref, wg_ref, wu_ref, wd_ref, gpost_ref, o_ref, hn_ref, acc_ref):
    j = pl.program_id(1)

    @pl.when(j == 0)
    def _():
        hn_ref[...] = _rms(x_ref[...], gpre_ref[...]).astype(bf16)

    hn = hn_ref[...]
    gate = jnp.dot(hn, wg_ref[...], preferred_element_type=f32)
    up = jnp.dot(hn, wu_ref[...], preferred_element_type=f32)
    act = (gate * jax.nn.sigmoid(gate) * up).astype(bf16)
    part = jnp.dot(act, wd_ref[...], preferred_element_type=f32)

    @pl.when(j == 0)
    def _():
        acc_ref[...] = part

    @pl.when(j > 0)
    def _():
        acc_ref[...] += part

    @pl.when(j == pl.num_programs(1) - 1)
    def _():
        o_ref[...] = x_ref[...] + _rms(acc_ref[...], gpost_ref[...])


def _ffn(x1, g_pre, wg_bf, wu_bf, wd_bf, g_post):
    m = x1.shape[0]
    return pl.pallas_call(
        _ffn_kernel,
        out_shape=jax.ShapeDtypeStruct((m, D_MODEL), f32),
        grid=(m // FFN_ROWS, D_FF // FFN_COLS),
        in_specs=[pl.BlockSpec((FFN_ROWS, D_MODEL), lambda i, j: (i, 0)),
                  pl.BlockSpec((1, D_MODEL), lambda i, j: (0, 0)),
                  pl.BlockSpec((D_MODEL, FFN_COLS), lambda i, j: (0, j)),
                  pl.BlockSpec((D_MODEL, FFN_COLS), lambda i, j: (0, j)),
                  pl.BlockSpec((FFN_COLS, D_MODEL), lambda i, j: (j, 0)),
                  pl.BlockSpec((1, D_MODEL), lambda i, j: (0, 0))],
        out_specs=pl.BlockSpec((FFN_ROWS, D_MODEL), lambda i, j: (i, 0)),
        scratch_shapes=[pltpu.VMEM((FFN_ROWS, D_MODEL), bf16),
                        pltpu.VMEM((FFN_ROWS, D_MODEL), f32)],
        compiler_params=_params("arbitrary", "arbitrary"),
        name="ffn",
    )(x1, g_pre, wg_bf, wu_bf, wd_bf, g_post)


def _interleave_glu_columns(w_conv):
    half = PROJ_COLS // 2
    d = w_conv.shape[0]
    return w_conv.reshape(d, 2, CONV_WIDTH // half, half).transpose(0, 2, 1, 3).reshape(d, 2 * CONV_WIDTH)


def kernel(x, w_in, lambda_q1, lambda_k1, lambda_q2, lambda_k2, subln_g, conv_dw_w, conv_dw_b,
           conv_ln_g, conv_ln_b, w_out, rel_bias, norm_pre_mix, norm_post_mix, norm_pre_ffn,
           norm_post_ffn, w_gate, w_up, w_down):
    batch, seq, _ = x.shape
    assert w_in.shape[0] == 1, "single layer"
    x2 = x.reshape(batch * seq, D_MODEL)

    w_in_l = w_in[0]
    qkv_cols = 2 * QK_WIDTH + DIFF_WIDTH
    w_in_bf = jnp.concatenate([w_in_l[:, :qkv_cols], _interleave_glu_columns(w_in_l[:, qkv_cols:])],
                              axis=1).astype(bf16)
    lamv = jnp.concatenate([lambda_q1, lambda_k1, lambda_q2, lambda_k2], axis=0)

    e_tiles = _bias_tiles(rel_bias)
    qkv, glu = _in_proj(x2, norm_pre_mix, w_in_bf)
    o_attn = _attention(qkv, e_tiles, lamv, subln_g, batch, seq)
    o_conv = _conv(glu, conv_dw_w[0], conv_dw_b, conv_ln_g, conv_ln_b, seq)
    x1 = _out_proj(o_attn, o_conv, w_out[0].astype(bf16), x2, norm_post_mix)
    out = _ffn(x1, norm_pre_ffn, w_gate[0].astype(bf16), w_up[0].astype(bf16),
               w_down[0].astype(bf16), norm_post_ffn)
    return out.reshape(batch, seq, D_MODEL)
```

```python
import functools
import math

import jax
import jax.numpy as jnp
from jax import lax
from jax.experimental import pallas as pl
from jax.experimental.pallas import tpu as pltpu

D_MODEL = 2048
DIFF_HEADS = 8
DIFF_HEAD_DIM = 64
DIFF_V_DIM = 2 * DIFF_HEAD_DIM
DIFF_WIDTH = DIFF_HEADS * DIFF_V_DIM
QK_WIDTH = DIFF_HEADS * 2 * DIFF_HEAD_DIM
CONV_WIDTH = D_MODEL - DIFF_WIDTH
CONV_KERNEL = 31
D_FF = 5632
REL_BUCKETS = 32
REL_MAX_DIST = 128
EPS = 1e-6
MASK_VALUE = -1e30
LAM_INIT = 0.8 - 0.6 * math.exp(-0.3 * 0)

LANES = 128
SUBLANES = 8
VMEM_LIMIT = 56 * 1024 * 1024

ATT_TILE = 256
ATT_HEADS = 4
PROJ_ROWS = 1024
PROJ_COLS = 1024
CONV_ROWS = 512
CONV_CHUNK = 32
CONV_HALO = 32
OUT_ROWS = 512
FFN_ROWS = 512
FFN_COLS = 512

N_CTILES = CONV_WIDTH // LANES

f32 = jnp.float32
bf16 = jnp.bfloat16


def _params(*sem):
    return pltpu.CompilerParams(dimension_semantics=sem, vmem_limit_bytes=VMEM_LIMIT)


def _rms(v, g):
    return v * lax.rsqrt(jnp.mean(v * v, axis=-1, keepdims=True) + EPS) * g


def _bias_kernel(rel_ref, e_ref):
    h = pl.program_id(0)
    t = e_ref.shape[1] // 2
    r = lax.broadcasted_iota(jnp.int32, (2 * t, 2 * t), 0)
    c = lax.broadcasted_iota(jnp.int32, (2 * t, 2 * t), 1)
    n = jnp.where(c >= t, c - t, c) - r + t
    nn = jnp.maximum(n, 0)
    max_exact = REL_BUCKETS // 2
    nf = jnp.maximum(nn, 1).astype(f32)
    large = max_exact + (jnp.log(nf * (1.0 / max_exact)) / math.log(REL_MAX_DIST / max_exact)
                         * (REL_BUCKETS - max_exact)).astype(jnp.int32)
    large = jnp.minimum(large, REL_BUCKETS - 1)
    bucket = jnp.where(nn < max_exact, nn, large)
    val = jnp.full((2 * t, 2 * t), rel_ref[0, h], f32)
    for bk in range(1, REL_BUCKETS):
        val = jnp.where(bucket == bk, rel_ref[bk, h], val)
    val = val - rel_ref[REL_BUCKETS - 1, h]
    e_ref[0] = jnp.where(n < 0, MASK_VALUE, val)


def _bias_tiles(rel_bias):
    t = ATT_TILE
    assert t >= REL_MAX_DIST
    return pl.pallas_call(
        _bias_kernel,
        out_shape=jax.ShapeDtypeStruct((DIFF_HEADS, 2 * t, 2 * t), f32),
        grid=(DIFF_HEADS,),
        in_specs=[pl.BlockSpec(memory_space=pltpu.SMEM)],
        out_specs=pl.BlockSpec((1, 2 * t, 2 * t), lambda h: (h, 0, 0)),
        compiler_params=_params("arbitrary"),
        name="bias_tiles",
    )(rel_bias)


def _in_proj_kernel(x_ref, g_ref, w_ref, qkv_ref, glu_ref, hn_ref):
    j = pl.program_id(1)

    @pl.when(j == 0)
    def _():
        hn_ref[...] = _rms(x_ref[...], g_ref[...]).astype(bf16)

    def proj():
        return jnp.dot(hn_ref[...], w_ref[...], preferred_element_type=f32)

    @pl.when(j == 0)
    def _():
        qkv_ref[...] = (proj() * (DIFF_HEAD_DIM ** -0.5)).astype(bf16)

    @pl.when(jnp.logical_or(j == 1, j == 2))
    def _():
        qkv_ref[...] = proj().astype(bf16)

    @pl.when(j >= 3)
    def _():
        y = proj()
        half = PROJ_COLS // 2
        glu = y[:, :half] * jax.nn.sigmoid(y[:, half:])
        for c in range(half // LANES):
            glu_ref[c] = glu[:, c * LANES:(c + 1) * LANES]


def _in_proj(x2, g, w_in_bf):
    m = x2.shape[0]
    n_chunks = w_in_bf.shape[1] // PROJ_COLS
    half_tiles = PROJ_COLS // 2 // LANES
    return pl.pallas_call(
        _in_proj_kernel,
        out_shape=(jax.ShapeDtypeStruct((m, 3 * QK_WIDTH), bf16),
                   jax.ShapeDtypeStruct((N_CTILES, m, LANES), f32)),
        grid=(m // PROJ_ROWS, n_chunks),
        in_specs=[pl.BlockSpec((PROJ_ROWS, D_MODEL), lambda i, j: (i, 0)),
                  pl.BlockSpec((1, D_MODEL), lambda i, j: (0, 0)),
                  pl.BlockSpec((D_MODEL, PROJ_COLS), lambda i, j: (0, j))],
        out_specs=(pl.BlockSpec((PROJ_ROWS, PROJ_COLS), lambda i, j: (i, jnp.minimum(j, 2))),
                   pl.BlockSpec((half_tiles, PROJ_ROWS, LANES),
                                lambda i, j: (jnp.maximum(j - 3, 0), i, 0))),
        scratch_shapes=[pltpu.VMEM((PROJ_ROWS, D_MODEL), bf16)],
        compiler_params=_params("arbitrary", "arbitrary"),
        name="in_proj",
    )(x2, g, w_in_bf)


def _attn_kernel(q_ref, k_ref, v_ref, e_ref, lamv_ref, subg_ref, o_ref, vt_ref, s_ref, acc_ref):
    t = ATT_TILE
    i = pl.program_id(2)
    heads = range(ATT_HEADS)

    def lanes(g):
        return slice(g * LANES, (g + 1) * LANES)

    @pl.when(i == 0)
    def _():
        for g in heads:
            for jj in range(vt_ref.shape[1]):
                vt_ref[g, jj] = v_ref[jj * t:(jj + 1) * t, lanes(g)].T

    def masked_qt(g):
        qt = q_ref[:, lanes(g)].T
        row = lax.broadcasted_iota(jnp.int32, qt.shape, 0)
        zero = jnp.zeros_like(qt)
        return jnp.concatenate([jnp.where(row < DIFF_HEAD_DIM, qt, zero),
                                jnp.where(row >= DIFF_HEAD_DIM, qt, zero)], axis=1)

    qqt = [masked_qt(g) for g in heads]

    def scores(g, j):
        start = pl.multiple_of(j * t, t)
        return jnp.dot(k_ref[pl.ds(start, t), lanes(g)], qqt[g], preferred_element_type=f32)

    def update(g, j, s, m, l):
        m_new = jnp.maximum(m, jnp.max(s, axis=0, keepdims=True))
        alpha = jnp.exp(m - m_new)
        p = jnp.exp(s - m_new)
        l = alpha * l + jnp.sum(p, axis=0, keepdims=True)
        pv = jnp.dot(vt_ref[g, j], p.astype(bf16), preferred_element_type=f32)
        acc_ref[g] = alpha * acc_ref[g] + pv
        return m_new, l

    def diag_tile(g):
        s = s_ref[0, g] + e_ref[g, t:, :]
        m = jnp.max(s, axis=0, keepdims=True)
        p = jnp.exp(s - m)
        l = jnp.sum(p, axis=0, keepdims=True)
        acc_ref[g] = jnp.dot(vt_ref[g, i], p.astype(bf16), preferred_element_type=f32)
        return m, l

    def issue_scores(slot, j):
        for g in heads:
            s_ref[slot, g] = scores(g, j)

    def consume(slot, j, state):
        return tuple(update(g, j, s_ref[slot, g], *state[g]) for g in heads)

    def far_pair(jj, state):
        j = 2 * jj
        issue_scores(1, j + 1)
        state = consume(0, j, state)
        issue_scores(0, j + 2)
        return consume(1, j + 1, state)

    n_far = jnp.maximum(i - 1, 0)
    jp = jnp.maximum(i - 1, 0)
    issue_scores(0, i)
    issue_scores(1, jp)
    state = tuple(diag_tile(g) for g in heads)
    issue_scores(0, 0)
    state = tuple(update(g, jp, s_ref[1, g] + jnp.where(i == 0, MASK_VALUE, e_ref[g, :t, :]), *state[g])
                  for g in heads)
    state = lax.fori_loop(0, n_far // 2, far_pair, state)
    state = lax.cond(n_far % 2 == 1, lambda st: consume(0, n_far - 1, st), lambda st: st, state)

    lv = lamv_ref[...]
    lam = (jnp.exp(jnp.sum(lv[0:1] * lv[1:2], axis=-1, keepdims=True))
           - jnp.exp(jnp.sum(lv[2:3] * lv[3:4], axis=-1, keepdims=True)) + LAM_INIT)
    for g in heads:
        _, l = state[g]
        ot = acc_ref[g] / l
        o = (ot[:, :t] - lam * ot[:, t:]).T
        o_ref[:, lanes(g)] = (_rms(o, subg_ref[...]) * (1.0 - LAM_INIT)).astype(bf16)


def _attention(qkv, e_tiles, lamv, subg, batch, seq):
    t = ATT_TILE
    nq = seq // t
    gw = ATT_HEADS * LANES
    groups = DIFF_HEADS // ATT_HEADS
    return pl.pallas_call(
        _attn_kernel,
        out_shape=jax.ShapeDtypeStruct((batch * seq, DIFF_WIDTH), bf16),
        grid=(batch, groups, nq),
        in_specs=[pl.BlockSpec((t, gw), lambda b, h, i: (b * nq + i, h)),
                  pl.BlockSpec((seq, gw), lambda b, h, i: (b, groups + h)),
                  pl.BlockSpec((seq, gw), lambda b, h, i: (b, 2 * groups + h)),
                  pl.BlockSpec((ATT_HEADS, 2 * t, 2 * t), lambda b, h, i: (h, 0, 0)),
                  pl.BlockSpec((4, DIFF_HEAD_DIM), lambda b, h, i: (0, 0)),
                  pl.BlockSpec((1, DIFF_V_DIM), lambda b, h, i: (0, 0))],
        out_specs=pl.BlockSpec((t, gw), lambda b, h, i: (b * nq + i, h)),
        scratch_shapes=[pltpu.VMEM((ATT_HEADS, nq, DIFF_V_DIM, t), bf16),
                        pltpu.VMEM((2, ATT_HEADS, t, 2 * t), f32),
                        pltpu.VMEM((ATT_HEADS, DIFF_V_DIM, 2 * t), f32)],
        compiler_params=_params("arbitrary", "arbitrary", "arbitrary"),
        name="attention",
    )(qkv, qkv, qkv, e_tiles, lamv, subg)


def _conv_kernel(main_ref, halo_ref, w_ref, b_ref, g_ref, beta_ref, o_ref, buf_ref, *, tiles_per_seq):
    i = pl.program_id(0)
    first = (i % tiles_per_seq) == 0
    halo = halo_ref[...]
    buf_ref[:, :CONV_HALO, :] = jnp.where(first, jnp.zeros_like(halo), halo)
    buf_ref[:, CONV_HALO:, :] = main_ref[...]
    lead = CONV_HALO - (CONV_KERNEL - 1)

    def chunk(r, _):
        base = pl.multiple_of(r * CONV_CHUNK, CONV_CHUNK)
        cols = []
        for c in range(N_CTILES):
            acc = jnp.broadcast_to(b_ref[:, c * LANES:(c + 1) * LANES], (CONV_CHUNK, LANES))
            for tap in range(CONV_KERNEL):
                wt = w_ref[tap:tap + 1, c * LANES:(c + 1) * LANES]
                acc = acc + wt * buf_ref[c, pl.ds(base + lead + tap, CONV_CHUNK), :]
            cols.append(acc)
        y = jnp.concatenate(cols, axis=-1)
        mu = jnp.mean(y, axis=-1, keepdims=True)
        yc = y - mu
        var = jnp.mean(yc * yc, axis=-1, keepdims=True)
        z = yc * lax.rsqrt(var + EPS) * g_ref[...] + beta_ref[...]
        o_ref[pl.ds(base, CONV_CHUNK), :] = (z * jax.nn.sigmoid(z)).astype(bf16)
        return 0

    lax.fori_loop(0, CONV_ROWS // CONV_CHUNK, chunk, 0)


def _conv(glu, dw_w, dw_b, ln_g, ln_b, seq):
    m = glu.shape[1]
    halo_blocks = CONV_ROWS // CONV_HALO
    w_pad = jnp.pad(dw_w, ((0, 32 - CONV_KERNEL), (0, 0)))
    return pl.pallas_call(
        functools.partial(_conv_kernel, tiles_per_seq=seq // CONV_ROWS),
        out_shape=jax.ShapeDtypeStruct((m, CONV_WIDTH), bf16),
        grid=(m // CONV_ROWS,),
        in_specs=[pl.BlockSpec((N_CTILES, CONV_ROWS, LANES), lambda i: (0, i, 0)),
                  pl.BlockSpec((N_CTILES, CONV_HALO, LANES),
                               lambda i: (0, jnp.maximum(i * halo_blocks - 1, 0), 0)),
                  pl.BlockSpec((32, CONV_WIDTH), lambda i: (0, 0)),
                  pl.BlockSpec((1, CONV_WIDTH), lambda i: (0, 0)),
                  pl.BlockSpec((1, CONV_WIDTH), lambda i: (0, 0)),
                  pl.BlockSpec((1, CONV_WIDTH), lambda i: (0, 0))],
        out_specs=pl.BlockSpec((CONV_ROWS, CONV_WIDTH), lambda i: (i, 0)),
        scratch_shapes=[pltpu.VMEM((N_CTILES, CONV_HALO + CONV_ROWS, LANES), f32)],
        compiler_params=_params("arbitrary"),
        name="conv",
    )(glu, glu, w_pad, dw_b, ln_g, ln_b)


def _out_proj_kernel(oa_ref, oc_ref, wa_ref, wc_ref, x_ref, g_ref, o_ref):
    mix = (jnp.dot(oa_ref[...], wa_ref[...], preferred_element_type=f32)
           + jnp.dot(oc_ref[...], wc_ref[...], preferred_element_type=f32))
    o_ref[...] = x_ref[...] + _rms(mix, g_ref[...])


def _out_proj(o_attn, o_conv, w_out_bf, x2, g):
    m = x2.shape[0]
    return pl.pallas_call(
        _out_proj_kernel,
        out_shape=jax.ShapeDtypeStruct((m, D_MODEL), f32),
        grid=(m // OUT_ROWS,),
        in_specs=[pl.BlockSpec((OUT_ROWS, DIFF_WIDTH), lambda i: (i, 0)),
                  pl.BlockSpec((OUT_ROWS, CONV_WIDTH), lambda i: (i, 0)),
                  pl.BlockSpec((DIFF_WIDTH, D_MODEL), lambda i: (0, 0)),
                  pl.BlockSpec((CONV_WIDTH, D_MODEL), lambda i: (1, 0)),
                  pl.BlockSpec((OUT_ROWS, D_MODEL), lambda i: (i, 0)),
                  pl.BlockSpec((1, D_MODEL), lambda i: (0, 0))],
        out_specs=pl.BlockSpec((OUT_ROWS, D_MODEL), lambda i: (i, 0)),
        compiler_params=_params("arbitrary"),
        name="out_proj",
    )(o_attn, o_conv, w_out_bf, w_out_bf, x2, g)


def _ffn_kernel(x_ref, gpre_ref, wg_ref, wu_ref, wd_ref, gpost_ref, o_ref, hn_ref, acc_ref):
    j = pl.program_id(1)

    @pl.when(j == 0)
    def _():
        hn_ref[...] = _rms(x_ref[...], gpre_ref[...]).astype(bf16)

    hn = hn_ref[...]
    gate = jnp.dot(hn, wg_ref[...], preferred_element_type=f32)
    up = jnp.dot(hn, wu_ref[...], preferred_element_type=f32)
    act = (gate * jax.nn.sigmoid(gate) * up).astype(bf16)
    part = jnp.dot(act, wd_ref[...], preferred_element_type=f32)

    @pl.when(j == 0)
    def _():
        acc_ref[...] = part

    @pl.when(j > 0)
    def _():
        acc_ref[...] += part

    @pl.when(j == pl.num_programs(1) - 1)
    def _():
        o_ref[...] = x_ref[...] + _rms(acc_ref[...], gpost_ref[...])


def _ffn(x1, g_pre, wg_bf, wu_bf, wd_bf, g_post):
    m = x1.shape[0]
    return pl.pallas_call(
        _ffn_kernel,
        out_shape=jax.ShapeDtypeStruct((m, D_MODEL), f32),
        grid=(m // FFN_ROWS, D_FF // FFN_COLS),
        in_specs=[pl.BlockSpec((FFN_ROWS, D_MODEL), lambda i, j: (i, 0)),
                  pl.BlockSpec((1, D_MODEL), lambda i, j: (0, 0)),
                  pl.BlockSpec((D_MODEL, FFN_COLS), lambda i, j: (0, j)),
                  pl.BlockSpec((D_MODEL, FFN_COLS), lambda i, j: (0, j)),
                  pl.BlockSpec((FFN_COLS, D_MODEL), lambda i, j: (j, 0)),
                  pl.BlockSpec((1, D_MODEL), lambda i, j: (0, 0))],
        out_specs=pl.BlockSpec((FFN_ROWS, D_MODEL), lambda i, j: (i, 0)),
        scratch_shapes=[pltpu.VMEM((FFN_ROWS, D_MODEL), bf16),
                        pltpu.VMEM((FFN_ROWS, D_MODEL), f32)],
        compiler_params=_params("arbitrary", "arbitrary"),
        name="ffn",
    )(x1, g_pre, wg_bf, wu_bf, wd_bf, g_post)


def _interleave_glu_columns(w_conv):
    half = PROJ_COLS // 2
    d = w_conv.shape[0]
    return w_conv.reshape(d, 2, CONV_WIDTH // half, half).transpose(0, 2, 1, 3).reshape(d, 2 * CONV_WIDTH)


def kernel(x, w_in, lambda_q1, lambda_k1, lambda_q2, lambda_k2, subln_g, conv_dw_w, conv_dw_b,
           conv_ln_g, conv_ln_b, w_out, rel_bias, norm_pre_mix, norm_post_mix, norm_pre_ffn,
           norm_post_ffn, w_gate, w_up, w_down):
    batch, seq, _ = x.shape
    assert w_in.shape[0] == 1, "single layer"
    x2 = x.reshape(batch * seq, D_MODEL)

    w_in_l = w_in[0]
    qkv_cols = 2 * QK_WIDTH + DIFF_WIDTH
    w_in_bf = jnp.concatenate([w_in_l[:, :qkv_cols], _interleave_glu_columns(w_in_l[:, qkv_cols:])],
                              axis=1).astype(bf16)
    lamv = jnp.concatenate([lambda_q1, lambda_k1, lambda_q2, lambda_k2], axis=0)

    e_tiles = _bias_tiles(rel_bias)
    qkv, glu = _in_proj(x2, norm_pre_mix, w_in_bf)
    o_attn = _attention(qkv, e_tiles, lamv, subln_g, batch, seq)
    o_conv = _conv(glu, conv_dw_w[0], conv_dw_b, conv_ln_g, conv_ln_b, seq)
    x1 = _out_proj(o_attn, o_conv, w_out[0].astype(bf16), x2, norm_post_mix)
    out = _ffn(x1, norm_pre_ffn, w_gate[0].astype(bf16), w_up[0].astype(bf16),
               w_down[0].astype(bf16), norm_post_ffn)
    return out.reshape(batch, seq, D_MODEL)
```

```python
import functools
import math

import jax
import jax.numpy as jnp
from jax import lax
from jax.experimental import pallas as pl
from jax.experimental.pallas import tpu as pltpu

D_MODEL = 2048
DIFF_HEADS = 8
DIFF_HEAD_DIM = 64
DIFF_V_DIM = 2 * DIFF_HEAD_DIM
DIFF_WIDTH = DIFF_HEADS * DIFF_V_DIM
QK_WIDTH = DIFF_HEADS * 2 * DIFF_HEAD_DIM
CONV_WIDTH = D_MODEL - DIFF_WIDTH
CONV_KERNEL = 31
D_FF = 5632
REL_BUCKETS = 32
REL_MAX_DIST = 128
EPS = 1e-6
MASK_VALUE = -1e30
LAM_INIT = 0.8 - 0.6 * math.exp(-0.3 * 0)

LANES = 128
SUBLANES = 8
VMEM_LIMIT = 56 * 1024 * 1024

ATT_TILE = 256
ATT_HEADS = 4
PROJ_ROWS = 1024
PROJ_COLS = 1024
GLU_COLS = 512
CONV_ROWS = 512
CONV_CHUNK = 32
CONV_HALO = 32
OUT_ROWS = 512
FFN_ROWS = 512
FFN_COLS = 512

N_CTILES = CONV_WIDTH // LANES
N_QKV_STEPS = (2 * QK_WIDTH + DIFF_WIDTH) // PROJ_COLS

f32 = jnp.float32
bf16 = jnp.bfloat16


def _params(*sem):
    return pltpu.CompilerParams(dimension_semantics=sem, vmem_limit_bytes=VMEM_LIMIT)


def _rms(v, g):
    return v * lax.rsqrt(jnp.mean(v * v, axis=-1, keepdims=True) + EPS) * g


def _bias_kernel(rel_ref, e_ref):
    h = pl.program_id(0)
    t = e_ref.shape[1] // 2
    r = lax.broadcasted_iota(jnp.int32, (2 * t, 2 * t), 0)
    c = lax.broadcasted_iota(jnp.int32, (2 * t, 2 * t), 1)
    n = jnp.where(c >= t, c - t, c) - r + t
    nn = jnp.maximum(n, 0)
    max_exact = REL_BUCKETS // 2
    nf = jnp.maximum(nn, 1).astype(f32)
    large = max_exact + (jnp.log(nf * (1.0 / max_exact)) / math.log(REL_MAX_DIST / max_exact)
                         * (REL_BUCKETS - max_exact)).astype(jnp.int32)
    large = jnp.minimum(large, REL_BUCKETS - 1)
    bucket = jnp.where(nn < max_exact, nn, large)
    val = jnp.full((2 * t, 2 * t), rel_ref[0, h], f32)
    for bk in range(1, REL_BUCKETS):
        val = jnp.where(bucket == bk, rel_ref[bk, h], val)
    val = val - rel_ref[REL_BUCKETS - 1, h]
    e_ref[0] = jnp.where(n < 0, MASK_VALUE, val)


def _bias_tiles(rel_bias):
    t = ATT_TILE
    assert t >= REL_MAX_DIST
    return pl.pallas_call(
        _bias_kernel,
        out_shape=jax.ShapeDtypeStruct((DIFF_HEADS, 2 * t, 2 * t), f32),
        grid=(DIFF_HEADS,),
        in_specs=[pl.BlockSpec(memory_space=pltpu.SMEM)],
        out_specs=pl.BlockSpec((1, 2 * t, 2 * t), lambda h: (h, 0, 0)),
        compiler_params=_params("arbitrary"),
        name="bias_tiles",
    )(rel_bias)


def _in_proj_kernel(x_ref, g_ref, w_ref, wa_ref, wg_ref, qkv_ref, glu_ref, hn_ref):
    j = pl.program_id(1)

    @pl.when(j == 0)
    def _():
        hn_ref[...] = _rms(x_ref[...], g_ref[...]).astype(bf16)

    def proj(w):
        return jnp.dot(hn_ref[...], w[...], preferred_element_type=f32)

    @pl.when(j == 0)
    def _():
        qkv_ref[...] = (proj(w_ref) * (DIFF_HEAD_DIM ** -0.5)).astype(bf16)

    @pl.when(jnp.logical_and(j >= 1, j < N_QKV_STEPS))
    def _():
        qkv_ref[...] = proj(w_ref).astype(bf16)

    @pl.when(j >= N_QKV_STEPS)
    def _():
        glu = proj(wa_ref) * jax.nn.sigmoid(proj(wg_ref))
        for c in range(GLU_COLS // LANES):
            glu_ref[c] = glu[:, c * LANES:(c + 1) * LANES]


def _in_proj(x2, g, w_in_bf):
    m = x2.shape[0]
    glu_steps = CONV_WIDTH // GLU_COLS
    a_blk0 = (2 * QK_WIDTH + DIFF_WIDTH) // GLU_COLS
    g_blk0 = a_blk0 + glu_steps

    def glu_step(j):
        return jnp.maximum(j - N_QKV_STEPS, 0)

    return pl.pallas_call(
        _in_proj_kernel,
        out_shape=(jax.ShapeDtypeStruct((m, 3 * QK_WIDTH), bf16),
                   jax.ShapeDtypeStruct((N_CTILES, m, LANES), f32)),
        grid=(m // PROJ_ROWS, N_QKV_STEPS + glu_steps),
        in_specs=[pl.BlockSpec((PROJ_ROWS, D_MODEL), lambda i, j: (i, 0)),
                  pl.BlockSpec((1, D_MODEL), lambda i, j: (0, 0)),
                  pl.BlockSpec((D_MODEL, PROJ_COLS), lambda i, j: (0, jnp.minimum(j, N_QKV_STEPS - 1))),
                  pl.BlockSpec((D_MODEL, GLU_COLS), lambda i, j: (0, a_blk0 + glu_step(j))),
                  pl.BlockSpec((D_MODEL, GLU_COLS), lambda i, j: (0, g_blk0 + glu_step(j)))],
        out_specs=(pl.BlockSpec((PROJ_ROWS, PROJ_COLS), lambda i, j: (i, jnp.minimum(j, N_QKV_STEPS - 1))),
                   pl.BlockSpec((GLU_COLS // LANES, PROJ_ROWS, LANES),
                                lambda i, j: (glu_step(j), i, 0))),
        scratch_shapes=[pltpu.VMEM((PROJ_ROWS, D_MODEL), bf16)],
        compiler_params=_params("arbitrary", "arbitrary"),
        name="in_proj",
    )(x2, g, w_in_bf, w_in_bf, w_in_bf)


def _attn_kernel(q_ref, k_ref, v_ref, e_ref, lamv_ref, subg_ref, o_ref, vt_ref, s_ref, acc_ref):
    t = ATT_TILE
    i = pl.program_id(2)
    heads = range(ATT_HEADS)

    def lanes(g):
        return slice(g * LANES, (g + 1) * LANES)

    @pl.when(i == 0)
    def _():
        for g in heads:
            for jj in range(vt_ref.shape[1]):
                vt_ref[g, jj] = v_ref[jj * t:(jj + 1) * t, lanes(g)].T

    def masked_qt(g):
        qt = q_ref[:, lanes(g)].T
        row = lax.broadcasted_iota(jnp.int32, qt.shape, 0)
        zero = jnp.zeros_like(qt)
        return jnp.concatenate([jnp.where(row < DIFF_HEAD_DIM, qt, zero),
                                jnp.where(row >= DIFF_HEAD_DIM, qt, zero)], axis=1)

    qqt = [masked_qt(g) for g in heads]

    def scores(g, j):
        start = pl.multiple_of(j * t, t)
        return jnp.dot(k_ref[pl.ds(start, t), lanes(g)], qqt[g], preferred_element_type=f32)

    def update(g, j, s, m, l):
        m_new = jnp.maximum(m, jnp.max(s, axis=0, keepdims=True))
        alpha = jnp.exp(m - m_new)
        p = jnp.exp(s - m_new)
        l = alpha * l + jnp.sum(p, axis=0, keepdims=True)
        pv = jnp.dot(vt_ref[g, j], p.astype(bf16), preferred_element_type=f32)
        acc_ref[g] = alpha * acc_ref[g] + pv
        return m_new, l

    def diag_tile(g):
        s = s_ref[0, g] + e_ref[g, t:, :]
        m = jnp.max(s, axis=0, keepdims=True)
        p = jnp.exp(s - m)
        l = jnp.sum(p, axis=0, keepdims=True)
        acc_ref[g] = jnp.dot(vt_ref[g, i], p.astype(bf16), preferred_element_type=f32)
        return m, l

    def issue_scores(slot, j):
        for g in heads:
            s_ref[slot, g] = scores(g, j)

    def consume(slot, j, state):
        return tuple(update(g, j, s_ref[slot, g], *state[g]) for g in heads)

    def far_pair(jj, state):
        j = 2 * jj
        issue_scores(1, j + 1)
        state = consume(0, j, state)
        issue_scores(0, j + 2)
        return consume(1, j + 1, state)

    n_far = jnp.maximum(i - 1, 0)
    jp = jnp.maximum(i - 1, 0)
    issue_scores(0, i)
    issue_scores(1, jp)
    state = tuple(diag_tile(g) for g in heads)
    issue_scores(0, 0)
    state = tuple(update(g, jp, s_ref[1, g] + jnp.where(i == 0, MASK_VALUE, e_ref[g, :t, :]), *state[g])
                  for g in heads)
    state = lax.fori_loop(0, n_far // 2, far_pair, state)
    state = lax.cond(n_far % 2 == 1, lambda st: consume(0, n_far - 1, st), lambda st: st, state)

    lv = lamv_ref[...]
    lam = (jnp.exp(jnp.sum(lv[0:1] * lv[1:2], axis=-1, keepdims=True))
           - jnp.exp(jnp.sum(lv[2:3] * lv[3:4], axis=-1, keepdims=True)) + LAM_INIT)
    for g in heads:
        _, l = state[g]
        ot = acc_ref[g] / l
        o = (ot[:, :t] - lam * ot[:, t:]).T
        o_ref[:, lanes(g)] = (_rms(o, subg_ref[...]) * (1.0 - LAM_INIT)).astype(bf16)


def _attention(qkv, e_tiles, lamv, subg, batch, seq):
    t = ATT_TILE
    nq = seq // t
    gw = ATT_HEADS * LANES
    groups = DIFF_HEADS // ATT_HEADS
    return pl.pallas_call(
        _attn_kernel,
        out_shape=jax.ShapeDtypeStruct((batch * seq, DIFF_WIDTH), bf16),
        grid=(batch, groups, nq),
        in_specs=[pl.BlockSpec((t, gw), lambda b, h, i: (b * nq + i, h)),
                  pl.BlockSpec((seq, gw), lambda b, h, i: (b, groups + h)),
                  pl.BlockSpec((seq, gw), lambda b, h, i: (b, 2 * groups + h)),
                  pl.BlockSpec((ATT_HEADS, 2 * t, 2 * t), lambda b, h, i: (h, 0, 0)),
                  pl.BlockSpec((4, DIFF_HEAD_DIM), lambda b, h, i: (0, 0)),
                  pl.BlockSpec((1, DIFF_V_DIM), lambda b, h, i: (0, 0))],
        out_specs=pl.BlockSpec((t, gw), lambda b, h, i: (b * nq + i, h)),
        scratch_shapes=[pltpu.VMEM((ATT_HEADS, nq, DIFF_V_DIM, t), bf16),
                        pltpu.VMEM((2, ATT_HEADS, t, 2 * t), f32),
                        pltpu.VMEM((ATT_HEADS, DIFF_V_DIM, 2 * t), f32)],
        compiler_params=_params("arbitrary", "arbitrary", "arbitrary"),
        name="attention",
    )(qkv, qkv, qkv, e_tiles, lamv, subg)


def _conv_kernel(main_ref, halo_ref, w_ref, b_ref, g_ref, beta_ref, o_ref, buf_ref, *, tiles_per_seq):
    i = pl.program_id(0)
    first = (i % tiles_per_seq) == 0
    halo = halo_ref[...]
    buf_ref[:, :CONV_HALO, :] = jnp.where(first, jnp.zeros_like(halo), halo)
    buf_ref[:, CONV_HALO:, :] = main_ref[...]
    lead = CONV_HALO - (CONV_KERNEL - 1)

    def chunk(r, _):
        base = pl.multiple_of(r * CONV_CHUNK, CONV_CHUNK)
        cols = []
        for c in range(N_CTILES):
            acc = jnp.broadcast_to(b_ref[:, c * LANES:(c + 1) * LANES], (CONV_CHUNK, LANES))
            for tap in range(CONV_KERNEL):
                wt = w_ref[tap:tap + 1, c * LANES:(c + 1) * LANES]
                acc = acc + wt * buf_ref[c, pl.ds(base + lead + tap, CONV_CHUNK), :]
            cols.append(acc)
        y = jnp.concatenate(cols, axis=-1)
        mu = jnp.mean(y, axis=-1, keepdims=True)
        yc = y - mu
        var = jnp.mean(yc * yc, axis=-1, keepdims=True)
        z = yc * lax.rsqrt(var + EPS) * g_ref[...] + beta_ref[...]
        o_ref[pl.ds(base, CONV_CHUNK), :] = (z * jax.nn.sigmoid(z)).astype(bf16)
        return 0

    lax.fori_loop(0, CONV_ROWS // CONV_CHUNK, chunk, 0)


def _conv(glu, dw_w, dw_b, ln_g, ln_b, seq):
    m = glu.shape[1]
    halo_blocks = CONV_ROWS // CONV_HALO
    w_pad = jnp.pad(dw_w, ((0, 32 - CONV_KERNEL), (0, 0)))
    return pl.pallas_call(
        functools.partial(_conv_kernel, tiles_per_seq=seq // CONV_ROWS),
        out_shape=jax.ShapeDtypeStruct((m, CONV_WIDTH), bf16),
        grid=(m // CONV_ROWS,),
        in_specs=[pl.BlockSpec((N_CTILES, CONV_ROWS, LANES), lambda i: (0, i, 0)),
                  pl.BlockSpec((N_CTILES, CONV_HALO, LANES),
                               lambda i: (0, jnp.maximum(i * halo_blocks - 1, 0), 0)),
                  pl.BlockSpec((32, CONV_WIDTH), lambda i: (0, 0)),
                  pl.BlockSpec((1, CONV_WIDTH), lambda i: (0, 0)),
                  pl.BlockSpec((1, CONV_WIDTH), lambda i: (0, 0)),
                  pl.BlockSpec((1, CONV_WIDTH), lambda i: (0, 0))],
        out_specs=pl.BlockSpec((CONV_ROWS, CONV_WIDTH), lambda i: (i, 0)),
        scratch_shapes=[pltpu.VMEM((N_CTILES, CONV_HALO + CONV_ROWS, LANES), f32)],
        compiler_params=_params("arbitrary"),
        name="conv",
    )(glu, glu, w_pad, dw_b, ln_g, ln_b)


def _out_proj_kernel(oa_ref, oc_ref, wa_ref, wc_ref, x_ref, g_ref, o_ref):
    mix = (jnp.dot(oa_ref[...], wa_ref[...], preferred_element_type=f32)
           + jnp.dot(oc_ref[...], wc_ref[...], preferred_element_type=f32))
    o_ref[...] = x_ref[...] + _rms(mix, g_ref[...])


def _out_proj(o_attn, o_conv, w_out_bf, x2, g):
    m = x2.shape[0]
    return pl.pallas_call(
        _out_proj_kernel,
        out_shape=jax.ShapeDtypeStruct((m, D_MODEL), f32),
        grid=(m // OUT_ROWS,),
        in_specs=[pl.BlockSpec((OUT_ROWS, DIFF_WIDTH), lambda i: (i, 0)),
                  pl.BlockSpec((OUT_ROWS, CONV_WIDTH), lambda i: (i, 0)),
                  pl.BlockSpec((DIFF_WIDTH, D_MODEL), lambda i: (0, 0)),
                  pl.BlockSpec((CONV_WIDTH, D_MODEL), lambda i: (1, 0)),
                  pl.BlockSpec((OUT_ROWS, D_MODEL), lambda i: (i, 0)),
                  pl.BlockSpec((1, D_MODEL), lambda i: (0, 0))],
        out_specs=pl.BlockSpec((OUT_ROWS, D_MODEL), lambda i: (i, 0)),
        compiler_params=_params("arbitrary"),
        name="out_proj",
    )(o_attn, o_conv, w_out_bf, w_out_bf, x2, g)


def _ffn_kernel(x_ref, gpre_ref, wg_ref, wu_ref, wd_ref, gpost_ref, o_ref, hn_ref):
    j = pl.program_id(1)

    @pl.when(j == 0)
    def _():
        hn_ref[...] = _rms(x_ref[...], gpre_ref[...]).astype(bf16)
        o_ref[...] = jnp.zeros_like(o_ref)

    hn = hn_ref[...]
    gate = jnp.dot(hn, wg_ref[...], preferred_element_type=f32)
    up = jnp.dot(hn, wu_ref[...], preferred_element_type=f32)
    act = (gate * jax.nn.sigmoid(gate) * up).astype(bf16)
    o_ref[...] += jnp.dot(act, wd_ref[...], preferred_element_type=f32)

    @pl.when(j == pl.num_programs(1) - 1)
    def _():
        o_ref[...] = x_ref[...] + _rms(o_ref[...], gpost_ref[...])


def _ffn(x1, g_pre, wg_bf, wu_bf, wd_bf, g_post):
    m = x1.shape[0]
    return pl.pallas_call(
        _ffn_kernel,
        out_shape=jax.ShapeDtypeStruct((m, D_MODEL), f32),
        grid=(m // FFN_ROWS, D_FF // FFN_COLS),
        in_specs=[pl.BlockSpec((FFN_ROWS, D_MODEL), lambda i, j: (i, 0)),
                  pl.BlockSpec((1, D_MODEL), lambda i, j: (0, 0)),
                  pl.BlockSpec((D_MODEL, FFN_COLS), lambda i, j: (0, j)),
                  pl.BlockSpec((D_MODEL, FFN_COLS), lambda i, j: (0, j)),
                  pl.BlockSpec((FFN_COLS, D_MODEL), lambda i, j: (j, 0)),
                  pl.BlockSpec((1, D_MODEL), lambda i, j: (0, 0))],
        out_specs=pl.BlockSpec((FFN_ROWS, D_MODEL), lambda i, j: (i, 0)),
        scratch_shapes=[pltpu.VMEM((FFN_ROWS, D_MODEL), bf16)],
        compiler_params=_params("arbitrary", "arbitrary"),
        name="ffn",
    )(x1, g_pre, wg_bf, wu_bf, wd_bf, g_post)


def kernel(x, w_in, lambda_q1, lambda_k1, lambda_q2, lambda_k2, subln_g, conv_dw_w, conv_dw_b,
           conv_ln_g, conv_ln_b, w_out, rel_bias, norm_pre_mix, norm_post_mix, norm_pre_ffn,
           norm_post_ffn, w_gate, w_up, w_down):
    batch, seq, _ = x.shape
    assert w_in.shape[0] == 1, "single layer"
    x2 = x.reshape(batch * seq, D_MODEL)

    w_in_bf = w_in[0].astype(bf16)
    lamv = jnp.concatenate([lambda_q1, lambda_k1, lambda_q2, lambda_k2], axis=0)

    e_tiles = _bias_tiles(rel_bias)
    qkv, glu = _in_proj(x2, norm_pre_mix, w_in_bf)
    o_attn = _attention(qkv, e_tiles, lamv, subln_g, batch, seq)
    o_conv = _conv(glu, conv_dw_w[0], conv_dw_b, conv_ln_g, conv_ln_b, seq)
    x1 = _out_proj(o_attn, o_conv, w_out[0].astype(bf16), x2, norm_post_mix)
    out = _ffn(x1, norm_pre_ffn, w_gate[0].astype(bf16), w_up[0].astype(bf16),
               w_down[0].astype(bf16), norm_post_ffn)
    return out.reshape(batch, seq, D_MODEL)
```

```python
import functools
import math

import jax
import jax.numpy as jnp
from jax import lax
from jax.experimental import pallas as pl
from jax.experimental.pallas import tpu as pltpu

D_MODEL = 2048
DIFF_HEADS = 8
DIFF_HEAD_DIM = 64
DIFF_V_DIM = 2 * DIFF_HEAD_DIM
DIFF_WIDTH = DIFF_HEADS * DIFF_V_DIM
QK_WIDTH = DIFF_HEADS * 2 * DIFF_HEAD_DIM
CONV_WIDTH = D_MODEL - DIFF_WIDTH
CONV_KERNEL = 31
D_FF = 5632
REL_BUCKETS = 32
REL_MAX_DIST = 128
EPS = 1e-6
MASK_VALUE = -1e30
LAM_INIT = 0.8 - 0.6 * math.exp(-0.3 * 0)
LOG2E = math.log2(math.e)

LANES = 128
SUBLANES = 8
VMEM_LIMIT = 56 * 1024 * 1024

ATT_TILE = 256
ATT_ONES_ROWS = 16
ATT_HEADS = 4
PROJ_ROWS = 1024
PROJ_COLS = 1024
GLU_COLS = 512
CONV_ROWS = 512
CONV_CHUNK = 32
CONV_HALO = 32
OUT_ROWS = 512
FFN_ROWS = 512
FFN_COLS = 512

N_CTILES = CONV_WIDTH // LANES
N_QKV_STEPS = (2 * QK_WIDTH + DIFF_WIDTH) // PROJ_COLS

f32 = jnp.float32
bf16 = jnp.bfloat16


def _params(*sem):
    return pltpu.CompilerParams(dimension_semantics=sem, vmem_limit_bytes=VMEM_LIMIT)


def _rms(v, g):
    return v * lax.rsqrt(jnp.mean(v * v, axis=-1, keepdims=True) + EPS) * g


def _bias_kernel(rel_ref, e_ref):
    h = pl.program_id(0)
    t = e_ref.shape[1] // 2
    r = lax.broadcasted_iota(jnp.int32, (2 * t, 2 * t), 0)
    c = lax.broadcasted_iota(jnp.int32, (2 * t, 2 * t), 1)
    n = jnp.where(c >= t, c - t, c) - r + t
    nn = jnp.maximum(n, 0)
    max_exact = REL_BUCKETS // 2
    nf = jnp.maximum(nn, 1).astype(f32)
    large = max_exact + (jnp.log(nf * (1.0 / max_exact)) / math.log(REL_MAX_DIST / max_exact)
                         * (REL_BUCKETS - max_exact)).astype(jnp.int32)
    large = jnp.minimum(large, REL_BUCKETS - 1)
    bucket = jnp.where(nn < max_exact, nn, large)
    val = jnp.full((2 * t, 2 * t), rel_ref[0, h], f32)
    for bk in range(1, REL_BUCKETS):
        val = jnp.where(bucket == bk, rel_ref[bk, h], val)
    val = (val - rel_ref[REL_BUCKETS - 1, h]) * LOG2E
    e_ref[0] = jnp.where(n < 0, MASK_VALUE, val)


def _bias_tiles(rel_bias):
    t = ATT_TILE
    assert t >= REL_MAX_DIST
    return pl.pallas_call(
        _bias_kernel,
        out_shape=jax.ShapeDtypeStruct((DIFF_HEADS, 2 * t, 2 * t), f32),
        grid=(DIFF_HEADS,),
        in_specs=[pl.BlockSpec(memory_space=pltpu.SMEM)],
        out_specs=pl.BlockSpec((1, 2 * t, 2 * t), lambda h: (h, 0, 0)),
        compiler_params=_params("arbitrary"),
        name="bias_tiles",
    )(rel_bias)


def _in_proj_kernel(x_ref, g_ref, w_ref, wa_ref, wg_ref, qkv_ref, glu_ref, hn_ref):
    j = pl.program_id(1)

    @pl.when(j == 0)
    def _():
        hn_ref[...] = _rms(x_ref[...], g_ref[...]).astype(bf16)

    def proj(w):
        return jnp.dot(hn_ref[...], w[...], preferred_element_type=f32)

    @pl.when(j == 0)
    def _():
        qkv_ref[...] = (proj(w_ref) * (DIFF_HEAD_DIM ** -0.5 * LOG2E)).astype(bf16)

    @pl.when(jnp.logical_and(j >= 1, j < N_QKV_STEPS))
    def _():
        qkv_ref[...] = proj(w_ref).astype(bf16)

    @pl.when(j >= N_QKV_STEPS)
    def _():
        glu = proj(wa_ref) * jax.nn.sigmoid(proj(wg_ref))
        for c in range(GLU_COLS // LANES):
            glu_ref[c] = glu[:, c * LANES:(c + 1) * LANES]


def _in_proj(x2, g, w_in_bf):
    m = x2.shape[0]
    glu_steps = CONV_WIDTH // GLU_COLS
    a_blk0 = (2 * QK_WIDTH + DIFF_WIDTH) // GLU_COLS
    g_blk0 = a_blk0 + glu_steps

    def glu_step(j):
        return jnp.maximum(j - N_QKV_STEPS, 0)

    return pl.pallas_call(
        _in_proj_kernel,
        out_shape=(jax.ShapeDtypeStruct((m, 3 * QK_WIDTH), bf16),
                   jax.ShapeDtypeStruct((N_CTILES, m, LANES), f32)),
        grid=(m // PROJ_ROWS, N_QKV_STEPS + glu_steps),
        in_specs=[pl.BlockSpec((PROJ_ROWS, D_MODEL), lambda i, j: (i, 0)),
                  pl.BlockSpec((1, D_MODEL), lambda i, j: (0, 0)),
                  pl.BlockSpec((D_MODEL, PROJ_COLS), lambda i, j: (0, jnp.minimum(j, N_QKV_STEPS - 1))),
                  pl.BlockSpec((D_MODEL, GLU_COLS), lambda i, j: (0, a_blk0 + glu_step(j))),
                  pl.BlockSpec((D_MODEL, GLU_COLS), lambda i, j: (0, g_blk0 + glu_step(j)))],
        out_specs=(pl.BlockSpec((PROJ_ROWS, PROJ_COLS), lambda i, j: (i, jnp.minimum(j, N_QKV_STEPS - 1))),
                   pl.BlockSpec((GLU_COLS // LANES, PROJ_ROWS, LANES),
                                lambda i, j: (glu_step(j), i, 0))),
        scratch_shapes=[pltpu.VMEM((PROJ_ROWS, D_MODEL), bf16)],
        compiler_params=_params("arbitrary", "arbitrary"),
        name="in_proj",
    )(x2, g, w_in_bf, w_in_bf, w_in_bf)


def _attn_kernel(q_ref, k_ref, v_ref, e_ref, lamv_ref, subg_ref, o_ref, vt_ref, s_ref, acc_ref):
    t = ATT_TILE
    i = pl.program_id(2)
    heads = range(ATT_HEADS)

    def lanes(g):
        return slice(g * LANES, (g + 1) * LANES)

    @pl.when(i == 0)
    def _():
        ones = jnp.ones((ATT_ONES_ROWS, t), bf16)
        for g in heads:
            for jj in range(vt_ref.shape[1]):
                vt_ref[g, jj, :DIFF_V_DIM] = v_ref[jj * t:(jj + 1) * t, lanes(g)].T
                vt_ref[g, jj, DIFF_V_DIM:] = ones

    def masked_qt(g):
        qt = q_ref[:, lanes(g)].T
        row = lax.broadcasted_iota(jnp.int32, qt.shape, 0)
        zero = jnp.zeros_like(qt)
        return jnp.concatenate([jnp.where(row < DIFF_HEAD_DIM, qt, zero),
                                jnp.where(row >= DIFF_HEAD_DIM, qt, zero)], axis=1)

    qqt = [masked_qt(g) for g in heads]

    def scores(g, j):
        start = pl.multiple_of(j * t, t)
        return jnp.dot(k_ref[pl.ds(start, t), lanes(g)], qqt[g], preferred_element_type=f32)

    def update(g, j, s, m):
        m_new = jnp.maximum(m, jnp.max(s, axis=0, keepdims=True))
        alpha = jnp.exp2(m - m_new)
        p = jnp.exp2(s - m_new)
        pv = jnp.dot(vt_ref[g, j], p.astype(bf16), preferred_element_type=f32)
        acc_ref[g] = alpha * acc_ref[g] + pv
        return m_new

    def diag_tile(g):
        s = s_ref[0, g] + e_ref[g, t:, :]
        m = jnp.max(s, axis=0, keepdims=True)
        p = jnp.exp2(s - m)
        acc_ref[g] = jnp.dot(vt_ref[g, i], p.astype(bf16), preferred_element_type=f32)
        return m

    def issue_scores(slot, j):
        for g in heads:
            s_ref[slot, g] = scores(g, j)

    def consume(slot, j, state):
        return tuple(update(g, j, s_ref[slot, g], state[g]) for g in heads)

    def far_pair(jj, state):
        j = 2 * jj
        issue_scores(1, j + 1)
        state = consume(0, j, state)
        issue_scores(0, j + 2)
        return consume(1, j + 1, state)

    n_far = jnp.maximum(i - 1, 0)
    jp = jnp.maximum(i - 1, 0)
    issue_scores(0, i)
    issue_scores(1, jp)
    state = tuple(diag_tile(g) for g in heads)
    issue_scores(0, 0)
    state = tuple(update(g, jp, s_ref[1, g] + jnp.where(i == 0, MASK_VALUE, e_ref[g, :t, :]), state[g])
                  for g in heads)
    state = lax.fori_loop(0, n_far // 2, far_pair, state)
    state = lax.cond(n_far % 2 == 1, lambda st: consume(0, n_far - 1, st), lambda st: st, state)

    lv = lamv_ref[...]
    lam = (jnp.exp(jnp.sum(lv[0:1] * lv[1:2], axis=-1, keepdims=True))
           - jnp.exp(jnp.sum(lv[2:3] * lv[3:4], axis=-1, keepdims=True)) + LAM_INIT)
    for g in heads:
        ot = acc_ref[g, :DIFF_V_DIM] / acc_ref[g, DIFF_V_DIM:DIFF_V_DIM + 1]
        o = (ot[:, :t] - lam * ot[:, t:]).T
        o_ref[:, lanes(g)] = (_rms(o, subg_ref[...]) * (1.0 - LAM_INIT)).astype(bf16)


def _attention(qkv, e_tiles, lamv, subg, batch, seq):
    t = ATT_TILE
    nq = seq // t
    gw = ATT_HEADS * LANES
    groups = DIFF_HEADS // ATT_HEADS
    return pl.pallas_call(
        _attn_kernel,
        out_shape=jax.ShapeDtypeStruct((batch * seq, DIFF_WIDTH), bf16),
        grid=(batch, groups, nq),
        in_specs=[pl.BlockSpec((t, gw), lambda b, h, i: (b * nq + i, h)),
                  pl.BlockSpec((seq, gw), lambda b, h, i: (b, groups + h)),
                  pl.BlockSpec((seq, gw), lambda b, h, i: (b, 2 * groups + h)),
                  pl.BlockSpec((ATT_HEADS, 2 * t, 2 * t), lambda b, h, i: (h, 0, 0)),
                  pl.BlockSpec((4, DIFF_HEAD_DIM), lambda b, h, i: (0, 0)),
                  pl.BlockSpec((1, DIFF_V_DIM), lambda b, h, i: (0, 0))],
        out_specs=pl.BlockSpec((t, gw), lambda b, h, i: (b * nq + i, h)),
        scratch_shapes=[pltpu.VMEM((ATT_HEADS, nq, DIFF_V_DIM + ATT_ONES_ROWS, t), bf16),
                        pltpu.VMEM((2, ATT_HEADS, t, 2 * t), f32),
                        pltpu.VMEM((ATT_HEADS, DIFF_V_DIM + ATT_ONES_ROWS, 2 * t), f32)],
        compiler_params=_params("arbitrary", "arbitrary", "arbitrary"),
        name="attention",
    )(qkv, qkv, qkv, e_tiles, lamv, subg)


def _conv_kernel(main_ref, halo_ref, w_ref, b_ref, g_ref, beta_ref, o_ref, buf_ref, *, tiles_per_seq):
    i = pl.program_id(0)
    first = (i % tiles_per_seq) == 0
    halo = halo_ref[...]
    buf_ref[:, :CONV_HALO, :] = jnp.where(first, jnp.zeros_like(halo), halo)
    buf_ref[:, CONV_HALO:, :] = main_ref[...]
    lead = CONV_HALO - (CONV_KERNEL - 1)

    def chunk(r, _):
        base = pl.multiple_of(r * CONV_CHUNK, CONV_CHUNK)
        cols = []
        for c in range(N_CTILES):
            acc = jnp.broadcast_to(b_ref[:, c * LANES:(c + 1) * LANES], (CONV_CHUNK, LANES))
            for tap in range(CONV_KERNEL):
                wt = w_ref[tap:tap + 1, c * LANES:(c + 1) * LANES]
                acc = acc + wt * buf_ref[c, pl.ds(base + lead + tap, CONV_CHUNK), :]
            cols.append(acc)
        y = jnp.concatenate(cols, axis=-1)
        mu = jnp.mean(y, axis=-1, keepdims=True)
        yc = y - mu
        var = jnp.mean(yc * yc, axis=-1, keepdims=True)
        z = yc * lax.rsqrt(var + EPS) * g_ref[...] + beta_ref[...]
        o_ref[pl.ds(base, CONV_CHUNK), :] = (z * jax.nn.sigmoid(z)).astype(bf16)
        return 0

    lax.fori_loop(0, CONV_ROWS // CONV_CHUNK, chunk, 0)


def _conv(glu, dw_w, dw_b, ln_g, ln_b, seq):
    m = glu.shape[1]
    halo_blocks = CONV_ROWS // CONV_HALO
    w_pad = jnp.pad(dw_w, ((0, 32 - CONV_KERNEL), (0, 0)))
    return pl.pallas_call(
        functools.partial(_conv_kernel, tiles_per_seq=seq // CONV_ROWS),
        out_shape=jax.ShapeDtypeStruct((m, CONV_WIDTH), bf16),
        grid=(m // CONV_ROWS,),
        in_specs=[pl.BlockSpec((N_CTILES, CONV_ROWS, LANES), lambda i: (0, i, 0)),
                  pl.BlockSpec((N_CTILES, CONV_HALO, LANES),
                               lambda i: (0, jnp.maximum(i * halo_blocks - 1, 0), 0)),
                  pl.BlockSpec((32, CONV_WIDTH), lambda i: (0, 0)),
                  pl.BlockSpec((1, CONV_WIDTH), lambda i: (0, 0)),
                  pl.BlockSpec((1, CONV_WIDTH), lambda i: (0, 0)),
                  pl.BlockSpec((1, CONV_WIDTH), lambda i: (0, 0))],
        out_specs=pl.BlockSpec((CONV_ROWS, CONV_WIDTH), lambda i: (i, 0)),
        scratch_shapes=[pltpu.VMEM((N_CTILES, CONV_HALO + CONV_ROWS, LANES), f32)],
        compiler_params=_params("arbitrary"),
        name="conv",
    )(glu, glu, w_pad, dw_b, ln_g, ln_b)


def _out_proj_kernel(oa_ref, oc_ref, wa_ref, wc_ref, x_ref, g_ref, o_ref):
    mix = (jnp.dot(oa_ref[...], wa_ref[...], preferred_element_type=f32)
           + jnp.dot(oc_ref[...], wc_ref[...], preferred_element_type=f32))
    o_ref[...] = x_ref[...] + _rms(mix, g_ref[...])


def _out_proj(o_attn, o_conv, w_out_bf, x2, g):
    m = x2.shape[0]
    return pl.pallas_call(
        _out_proj_kernel,
        out_shape=jax.ShapeDtypeStruct((m, D_MODEL), f32),
        grid=(m // OUT_ROWS,),
        in_specs=[pl.BlockSpec((OUT_ROWS, DIFF_WIDTH), lambda i: (i, 0)),
                  pl.BlockSpec((OUT_ROWS, CONV_WIDTH), lambda i: (i, 0)),
                  pl.BlockSpec((DIFF_WIDTH, D_MODEL), lambda i: (0, 0)),
                  pl.BlockSpec((CONV_WIDTH, D_MODEL), lambda i: (1, 0)),
                  pl.BlockSpec((OUT_ROWS, D_MODEL), lambda i: (i, 0)),
                  pl.BlockSpec((1, D_MODEL), lambda i: (0, 0))],
        out_specs=pl.BlockSpec((OUT_ROWS, D_MODEL), lambda i: (i, 0)),
        compiler_params=_params("arbitrary"),
        name="out_proj",
    )(o_attn, o_conv, w_out_bf, w_out_bf, x2, g)


def _ffn_kernel(x_ref, gpre_ref, wg_ref, wu_ref, wd_ref, gpost_ref, o_ref, hn_ref):
    j = pl.program_id(1)

    @pl.when(j == 0)
    def _():
        hn_ref[...] = _rms(x_ref[...], gpre_ref[...]).astype(bf16)
        o_ref[...] = jnp.zeros_like(o_ref)

    hn = hn_ref[...]
    gate = jnp.dot(hn, wg_ref[...], preferred_element_type=f32)
    up = jnp.dot(hn, wu_ref[...], preferred_element_type=f32)
    act = (gate * jax.nn.sigmoid(gate) * up).astype(bf16)
    o_ref[...] += jnp.dot(act, wd_ref[...], preferred_element_type=f32)

    @pl.when(j == pl.num_programs(1) - 1)
    def _():
        o_ref[...] = x_ref[...] + _rms(o_ref[...], gpost_ref[...])


def _ffn(x1, g_pre, wg_bf, wu_bf, wd_bf, g_post):
    m = x1.shape[0]
    return pl.pallas_call(
        _ffn_kernel,
        out_shape=jax.ShapeDtypeStruct((m, D_MODEL), f32),
        grid=(m // FFN_ROWS, D_FF // FFN_COLS),
        in_specs=[pl.BlockSpec((FFN_ROWS, D_MODEL), lambda i, j: (i, 0)),
                  pl.BlockSpec((1, D_MODEL), lambda i, j: (0, 0)),
                  pl.BlockSpec((D_MODEL, FFN_COLS), lambda i, j: (0, j)),
                  pl.BlockSpec((D_MODEL, FFN_COLS), lambda i, j: (0, j)),
                  pl.BlockSpec((FFN_COLS, D_MODEL), lambda i, j: (j, 0)),
                  pl.BlockSpec((1, D_MODEL), lambda i, j: (0, 0))],
        out_specs=pl.BlockSpec((FFN_ROWS, D_MODEL), lambda i, j: (i, 0)),
        scratch_shapes=[pltpu.VMEM((FFN_ROWS, D_MODEL), bf16)],
        compiler_params=_params("arbitrary", "arbitrary"),
        name="ffn",
    )(x1, g_pre, wg_bf, wu_bf, wd_bf, g_post)


def kernel(x, w_in, lambda_q1, lambda_k1, lambda_q2, lambda_k2, subln_g, conv_dw_w, conv_dw_b,
           conv_ln_g, conv_ln_b, w_out, rel_bias, norm_pre_mix, norm_post_mix, norm_pre_ffn,
           norm_post_ffn, w_gate, w_up, w_down):
    batch, seq, _ = x.shape
    assert w_in.shape[0] == 1, "single layer"
    x2 = x.reshape(batch * seq, D_MODEL)

    w_in_bf = w_in[0].astype(bf16)
    lamv = jnp.concatenate([lambda_q1, lambda_k1, lambda_q2, lambda_k2], axis=0)

    e_tiles = _bias_tiles(rel_bias)
    qkv, glu = _in_proj(x2, norm_pre_mix, w_in_bf)
    o_attn = _attention(qkv, e_tiles, lamv, subln_g, batch, seq)
    o_conv = _conv(glu, conv_dw_w[0], conv_dw_b, conv_ln_g, conv_ln_b, seq)
    x1 = _out_proj(o_attn, o_conv, w_out[0].astype(bf16), x2, norm_post_mix)
    out = _ffn(x1, norm_pre_ffn, w_gate[0].astype(bf16), w_up[0].astype(bf16),
               w_down[0].astype(bf16), norm_post_ffn)
    return out.reshape(batch, seq, D_MODEL)
```
